```python
import math
import jax, jax.numpy as jnp
from jax import lax
import numpy as np

D_MODEL = 1024
BATCH = 8
SEQ = 2048
DEPTH = 1

D_CONF = D_MODEL // 2
D_SC = D_MODEL // 2
GROUP = 128
CONF_KERNEL = 31
SC_KERNEL = 3
N_BRANCH = 2
D_FF = int(math.ceil(8 * D_MODEL / 3 / 256) * 256)
D_IN = 2 * D_CONF + 3 * D_SC + N_BRANCH * D_MODEL
RMS_EPS = 1e-6
LN_EPS = 1e-5

kernel_name = "hybrid_conformer_shortconv_gated_adaln_block"


def _rmsnorm(x, g):
    xf = x.astype(jnp.float32)
    y = xf * lax.rsqrt(jnp.mean(xf * xf, axis=-1, keepdims=True) + RMS_EPS)
    return (y * g.astype(jnp.float32)).astype(x.dtype)


def _layernorm(x, g, b):
    xf = x.astype(jnp.float32)
    mu = jnp.mean(xf, axis=-1, keepdims=True)
    var = jnp.mean(jnp.square(xf - mu), axis=-1, keepdims=True)
    y = (xf - mu) * lax.rsqrt(var + LN_EPS)
    return (y * g.astype(jnp.float32) + b.astype(jnp.float32)).astype(x.dtype)


def _causal_dwconv(x, w):
    k, ch = w.shape
    rhs = w.astype(x.dtype)[:, None, :]
    return lax.conv_general_dilated(
        x, rhs, window_strides=(1,), padding=[(k - 1, 0)],
        dimension_numbers=("NWC", "WIO", "NWC"), feature_group_count=ch)


def _modulate(h, shift, scale):
    return h * (1.0 + scale[:, None, :]) + shift[:, None, :]


def setup_inputs(seed: int = 0) -> dict:
    key = jax.random.key(seed)
    ks = jax.random.split(key, 24)
    L, D = DEPTH, D_MODEL
    n = lambda k, shape, s: (jax.random.normal(k, shape, jnp.float32) * s)
    return {
        "x": n(ks[0], (BATCH, SEQ, D), 1.0),
        "c": n(ks[1], (BATCH, D), 1.0),
        "w_ada": n(ks[2], (L, D, 6 * D), 0.3 * D ** -0.5),
        "b_ada": n(ks[3], (L, 6 * D), 0.02),
        "norm1_g": 1.0 + n(ks[4], (L, D), 0.02),
        "w_in": n(ks[5], (L, D, D_IN), D ** -0.5),
        "b_glu": n(ks[6], (L, 2 * D_CONF), 0.02),
        "conf_dw_w": n(ks[7], (L, CONF_KERNEL, D_CONF), CONF_KERNEL ** -0.5),
        "conf_dw_b": n(ks[8], (L, D_CONF), 0.02),
        "conf_ln_g": 1.0 + n(ks[9], (L, D_CONF), 0.02),
        "conf_ln_b": n(ks[10], (L, D_CONF), 0.02),
        "conf_w_out": n(ks[11], (L, D_CONF, D), D_CONF ** -0.5),
        "conf_b_out": n(ks[12], (L, D), 0.02),
        "sc_dw_w": n(ks[13], (L, SC_KERNEL, D_SC), SC_KERNEL ** -0.5),
        "sc_w_out": n(ks[14], (L, D_SC, D), D_SC ** -0.5),
        "b_gate": n(ks[15], (L, N_BRANCH * D), 0.02),
        "w_o": n(ks[16], (L, D, D), D ** -0.5),
        "norm2_g": 1.0 + n(ks[17], (L, D), 0.02),
        "w_gu": n(ks[18], (L, D, 2 * D_FF), D ** -0.5),
        "w_down": n(ks[19], (L, D_FF, D), D_FF ** -0.5),
        "final_g": 1.0 + n(ks[20], (D,), 0.02),
    }


def reference(x, c, w_ada, b_ada, norm1_g, w_in, b_glu, conf_dw_w, conf_dw_b,
              conf_ln_g, conf_ln_b, conf_w_out, conf_b_out, sc_dw_w, sc_w_out,
              b_gate, w_o, norm2_g, w_gu, w_down, final_g):
    c_act = jax.nn.silu(c)
    for l in range(DEPTH):
        mod = c_act @ w_ada[l] + b_ada[l]
        sh1, sc1, gt1, sh2, sc2, gt2 = jnp.split(mod, 6, axis=-1)

        h = _modulate(_rmsnorm(x, norm1_g[l]), sh1, sc1)
        p = h @ w_in[l]
        conf_in, sc_in, gate_in = jnp.split(
            p, [2 * D_CONF, 2 * D_CONF + 3 * D_SC], axis=-1)

        u = conf_in + b_glu[l]
        u_a, u_b = jnp.split(u, 2, axis=-1)
        u = u_a * jax.nn.sigmoid(u_b)
        u = _causal_dwconv(u, conf_dw_w[l]) + conf_dw_b[l]
        u = jax.nn.silu(_layernorm(u, conf_ln_g[l], conf_ln_b[l]))
        y_a = u @ conf_w_out[l] + conf_b_out[l]

        g_b, g_c, v = jnp.split(sc_in, 3, axis=-1)
        v = g_b * _causal_dwconv(g_c * v, sc_dw_w[l])
        y_b = v @ sc_w_out[l]

        gates = jax.nn.sigmoid(gate_in + b_gate[l])
        ga, gb = jnp.split(gates, 2, axis=-1)
        mixed = (ga * y_a + gb * y_b) @ w_o[l]
        x = x + gt1[:, None, :] * mixed

        h2 = _modulate(_rmsnorm(x, norm2_g[l]), sh2, sc2)
        gu = h2 @ w_gu[l]
        f_g, f_u = jnp.split(gu, 2, axis=-1)
        ffn = (jax.nn.silu(f_g) * f_u) @ w_down[l]
        x = x + gt2[:, None, :] * ffn

    return _rmsnorm(x, final_g)
```

```python
import functools

import jax
import jax.numpy as jnp
from jax import lax
from jax.experimental import pallas as pl
from jax.experimental.pallas import tpu as pltpu

RMS_EPS = 1e-6
LN_EPS = 1e-5

SUBLANES = 8
TOKEN_TILE = 512
FFN_CHUNKS = (1024, 1024, 768)
VMEM_LIMIT_BYTES = 60 * 1024 * 1024


def _round_up(n, m):
    return (n + m - 1) // m * m


def _dot(a, b):
    return jnp.dot(a, b, preferred_element_type=jnp.float32)


def _sigmoid(x):
    return jax.nn.sigmoid(x)


def _rms_modulate(x, g, shift, scale):
    ms = jnp.mean(x * x, axis=-1, keepdims=True)
    y = x * lax.rsqrt(ms + RMS_EPS) * g
    return y * (1.0 + scale) + shift


def _causal_dwconv(buf_ref, w_ref, halo, tile):
    k_taps = w_ref.shape[0]
    acc = None
    for k in range(k_taps):
        start = halo - (k_taps - 1) + k
        term = buf_ref[pl.ds(start, tile), :] * w_ref[k:k + 1, :]
        acc = term if acc is None else acc + term
    return acc


def _mod_kernel(c_ref, w_ref, b_ref, o_ref):
    c = c_ref[...]
    c_act = c * _sigmoid(c)
    o_ref[...] = jnp.dot(c_act, w_ref[...], preferred_element_type=jnp.float32,
                         precision=lax.Precision.HIGHEST) + b_ref[...]


def _block_kernel(x_ref, mod_ref, n1g_ref, w_in_ref, b_glu_ref, cw_ref, cb_ref,
                  lng_ref, lnb_ref, cwo_ref, cbo_ref, sw_ref, swo_ref, bgate_ref,
                  wo_ref, n2g_ref, wgu_ref, wdn_ref, fg_ref, o_ref, ubuf, sbuf,
                  *, tile, d_model, d_conf, d_sc, d_ff, halo_c, halo_s):
    j = pl.program_id(1)

    @pl.when(j == 0)
    def _():
        ubuf[0:halo_c, :] = jnp.zeros((halo_c, d_conf), jnp.float32)
        sbuf[0:halo_s, :] = jnp.zeros((halo_s, d_sc), jnp.float32)

    x = x_ref[...]
    sh1, sc1, gt1 = mod_ref[0], mod_ref[1], mod_ref[2]
    sh2, sc2, gt2 = mod_ref[3], mod_ref[4], mod_ref[5]

    hb = _rms_modulate(x, n1g_ref[...], sh1, sc1).astype(jnp.bfloat16)

    conf_in = _dot(hb, w_in_ref[:, 0:2 * d_conf]) + b_glu_ref[...]
    u = conf_in[:, :d_conf] * _sigmoid(conf_in[:, d_conf:])
    ubuf[halo_c:halo_c + tile, :] = u
    conv = _causal_dwconv(ubuf, cw_ref, halo_c, tile) + cb_ref[...]
    ubuf[0:halo_c, :] = ubuf[tile:tile + halo_c, :]
    mu = jnp.mean(conv, axis=-1, keepdims=True)
    cen = conv - mu
    var = jnp.mean(cen * cen, axis=-1, keepdims=True)
    ln = cen * lax.rsqrt(var + LN_EPS) * lng_ref[...] + lnb_ref[...]
    act = (ln * _sigmoid(ln)).astype(jnp.bfloat16)
    y_a = _dot(act, cwo_ref[...]) + cbo_ref[...]

    off = 2 * d_conf
    sc_in = _dot(hb, w_in_ref[:, off:off + 3 * d_sc])
    g_b = sc_in[:, :d_sc]
    sbuf[halo_s:halo_s + tile, :] = sc_in[:, d_sc:2 * d_sc] * sc_in[:, 2 * d_sc:]
    conv_s = _causal_dwconv(sbuf, sw_ref, halo_s, tile)
    sbuf[0:halo_s, :] = sbuf[tile:tile + halo_s, :]
    y_b = _dot((g_b * conv_s).astype(jnp.bfloat16), swo_ref[...])

    off = 2 * d_conf + 3 * d_sc
    gates = _sigmoid(_dot(hb, w_in_ref[:, off:off + 2 * d_model]) + bgate_ref[...])
    merged = gates[:, :d_model] * y_a + gates[:, d_model:] * y_b
    x1 = x + gt1 * _dot(merged.astype(jnp.bfloat16), wo_ref[...])

    h2 = _rms_modulate(x1, n2g_ref[...], sh2, sc2).astype(jnp.bfloat16)
    ffn = None
    start = 0
    for width in FFN_CHUNKS:
        f_g = _dot(h2, wgu_ref[:, start:start + width])
        f_u = _dot(h2, wgu_ref[:, d_ff + start:d_ff + start + width])
        a = (f_g * _sigmoid(f_g) * f_u).astype(jnp.bfloat16)
        part = _dot(a, wdn_ref[start:start + width, :])
        ffn = part if ffn is None else ffn + part
        start += width
    x2 = x1 + gt2 * ffn

    ms = jnp.mean(x2 * x2, axis=-1, keepdims=True)
    o_ref[...] = x2 * lax.rsqrt(ms + RMS_EPS) * fg_ref[...]


def _resident(shape):
    nd = len(shape)
    return pl.BlockSpec(shape, lambda b, j: (0,) * nd, pipeline_mode=pl.Buffered(1))


def _adaln_mod(c, w_ada, b_ada):
    batch, d_model = c.shape
    n_mod = w_ada.shape[1] // d_model
    return pl.pallas_call(
        _mod_kernel,
        grid=(n_mod,),
        in_specs=[
            pl.BlockSpec((batch, d_model), lambda i: (0, 0)),
            pl.BlockSpec((d_model, d_model), lambda i: (0, i)),
            pl.BlockSpec((1, d_model), lambda i: (0, i)),
        ],
        out_specs=pl.BlockSpec((batch, d_model), lambda i: (0, i)),
        out_shape=jax.ShapeDtypeStruct((batch, n_mod * d_model), jnp.float32),
        name="adaln_mod",
    )(c, w_ada, b_ada.reshape(1, -1))


def _block(x, mod, norm1_g, w_in, b_glu, conf_dw_w, conf_dw_b, conf_ln_g, conf_ln_b,
           conf_w_out, conf_b_out, sc_dw_w, sc_w_out, b_gate, w_o, norm2_g, w_gu,
           w_down, final_g):
    batch, seq, d_model = x.shape
    d_conf = conf_dw_w.shape[1]
    d_sc = sc_dw_w.shape[1]
    d_ff = w_down.shape[0]
    tile = TOKEN_TILE
    assert seq % tile == 0 and sum(FFN_CHUNKS) == d_ff
    halo_c = _round_up(conf_dw_w.shape[0] - 1, SUBLANES)
    halo_s = _round_up(sc_dw_w.shape[0] - 1, SUBLANES)

    bf = lambda w: w.astype(jnp.bfloat16)
    row = lambda v: v.reshape(1, -1)
    operands = [
        row(norm1_g), bf(w_in), row(b_glu), conf_dw_w, row(conf_dw_b), row(conf_ln_g),
        row(conf_ln_b), bf(conf_w_out), row(conf_b_out), sc_dw_w, bf(sc_w_out),
        row(b_gate), bf(w_o), row(norm2_g), bf(w_gu), bf(w_down), row(final_g),
    ]
    n_mod = mod.shape[1] // d_model
    mod4 = mod.reshape(batch, n_mod, 1, d_model)

    kern = functools.partial(
        _block_kernel, tile=tile, d_model=d_model, d_conf=d_conf, d_sc=d_sc,
        d_ff=d_ff, halo_c=halo_c, halo_s=halo_s)
    return pl.pallas_call(
        kern,
        grid=(batch, seq // tile),
        in_specs=[
            pl.BlockSpec((None, tile, d_model), lambda b, j: (b, j, 0)),
            pl.BlockSpec((None, n_mod, 1, d_model), lambda b, j: (b, 0, 0, 0)),
        ] + [_resident(op.shape) for op in operands],
        out_specs=pl.BlockSpec((None, tile, d_model), lambda b, j: (b, j, 0)),
        out_shape=jax.ShapeDtypeStruct(x.shape, x.dtype),
        scratch_shapes=[
            pltpu.VMEM((halo_c + tile, d_conf), jnp.float32),
            pltpu.VMEM((halo_s + tile, d_sc), jnp.float32),
        ],
        compiler_params=pltpu.CompilerParams(
            dimension_semantics=("arbitrary", "arbitrary"),
            vmem_limit_bytes=VMEM_LIMIT_BYTES),
        name="fused_block",
    )(x, mod4, *operands)


def kernel(x, c, w_ada, b_ada, norm1_g, w_in, b_glu, conf_dw_w, conf_dw_b, conf_ln_g,
           conf_ln_b, conf_w_out, conf_b_out, sc_dw_w, sc_w_out, b_gate, w_o, norm2_g,
           w_gu, w_down, final_g):
    depth = w_ada.shape[0]
    assert depth == 1, "final RMSNorm is fused into the (single) block"
    mod = _adaln_mod(c, w_ada[0], b_ada[0])
    return _block(x, mod, norm1_g[0], w_in[0], b_glu[0], conf_dw_w[0], conf_dw_b[0],
                  conf_ln_g[0], conf_ln_b[0], conf_w_out[0], conf_b_out[0], sc_dw_w[0],
                  sc_w_out[0], b_gate[0], w_o[0], norm2_g[0], w_gu[0], w_down[0],
                  final_g)
```

```python
import functools

import jax
import jax.numpy as jnp
from jax import lax
from jax.experimental import pallas as pl
from jax.experimental.pallas import tpu as pltpu

RMS_EPS = 1e-6
LN_EPS = 1e-5

SUBLANES = 8
TOKEN_TILE = 512
FFN_CHUNKS = (1024, 1024, 768)
VMEM_LIMIT_BYTES = 60 * 1024 * 1024


def _round_up(n, m):
    return (n + m - 1) // m * m


def _dot(a, b):
    return jnp.dot(a, b, preferred_element_type=jnp.float32)


def _sigmoid(x):
    return jax.nn.sigmoid(x)


def _rms_modulate(x, g, shift, scale):
    ms = jnp.mean(x * x, axis=-1, keepdims=True)
    y = x * lax.rsqrt(ms + RMS_EPS) * g
    return y * (1.0 + scale) + shift


def _causal_dwconv(buf_ref, w_ref, halo, tile):
    k_taps = w_ref.shape[0]
    base = halo - (k_taps - 1)
    out = None
    for r in range(SUBLANES):
        rows = tile + SUBLANES if r else tile
        acc = None
        for k in range(k_taps):
            if (base + k) % SUBLANES != r:
                continue
            term = buf_ref[pl.ds(base + k - r, rows), :] * w_ref[k:k + 1, :]
            acc = term if acc is None else acc + term
        if acc is None:
            continue
        shifted = acc[r:r + tile, :]
        out = shifted if out is None else out + shifted
    return out


def _mod_kernel(c_ref, w_ref, b_ref, o_ref):
    c = c_ref[...]
    c_act = c * _sigmoid(c)
    o_ref[...] = jnp.dot(c_act, w_ref[...], preferred_element_type=jnp.float32,
                         precision=lax.Precision.HIGHEST) + b_ref[...]


def _block_kernel(xa_ref, moda_ref, xb_ref, modb_ref, n1g_ref, w_in_ref, b_glu_ref,
                  cw_ref, cb_ref, lng_ref, lnb_ref, cwo_ref, cbo_ref, sw_ref, swo_ref,
                  bgate_ref, wo_ref, n2g_ref, wgu_ref, wdn_ref, fg_ref, o_ref,
                  ubuf, sbuf, act_ref,
                  *, tile, tiles_per_seq, n_tiles, d_model, d_conf, d_sc, d_ff,
                  halo_c, halo_s):
    s = pl.program_id(0)
    tile_a = jnp.minimum(s, n_tiles - 1)
    tile_b = jnp.maximum(s - 1, 0)

    @pl.when(s == 0)
    def _():
        act_ref[...] = jnp.zeros(act_ref.shape, act_ref.dtype)

    @pl.when(tile_a % tiles_per_seq == 0)
    def _():
        ubuf[0:halo_c, :] = jnp.zeros((halo_c, d_conf), jnp.float32)

    @pl.when(tile_b % tiles_per_seq == 0)
    def _():
        sbuf[0:halo_s, :] = jnp.zeros((halo_s, d_sc), jnp.float32)

    y_a = _dot(act_ref[...], cwo_ref[...]) + cbo_ref[...]

    xa = xa_ref[...]
    ha = _rms_modulate(xa, n1g_ref[...], moda_ref[0], moda_ref[1]).astype(jnp.bfloat16)
    conf_in = _dot(ha, w_in_ref[:, 0:2 * d_conf]) + b_glu_ref[...]
    ubuf[halo_c:halo_c + tile, :] = conf_in[:, :d_conf] * _sigmoid(conf_in[:, d_conf:])
    conv = _causal_dwconv(ubuf, cw_ref, halo_c, tile) + cb_ref[...]
    ubuf[0:halo_c, :] = ubuf[tile:tile + halo_c, :]
    mu = jnp.mean(conv, axis=-1, keepdims=True)
    cen = conv - mu
    var = jnp.mean(cen * cen, axis=-1, keepdims=True)
    ln = cen * lax.rsqrt(var + LN_EPS) * lng_ref[...] + lnb_ref[...]
    act_ref[...] = (ln * _sigmoid(ln)).astype(jnp.bfloat16)

    x = xb_ref[...]
    sh1, sc1, gt1 = modb_ref[0], modb_ref[1], modb_ref[2]
    sh2, sc2, gt2 = modb_ref[3], modb_ref[4], modb_ref[5]
    hb = _rms_modulate(x, n1g_ref[...], sh1, sc1).astype(jnp.bfloat16)

    off = 2 * d_conf
    sc_in = _dot(hb, w_in_ref[:, off:off + 3 * d_sc])
    g_b = sc_in[:, :d_sc]
    sbuf[halo_s:halo_s + tile, :] = sc_in[:, d_sc:2 * d_sc] * sc_in[:, 2 * d_sc:]
    conv_s = _causal_dwconv(sbuf, sw_ref, halo_s, tile)
    sbuf[0:halo_s, :] = sbuf[tile:tile + halo_s, :]
    y_b = _dot((g_b * conv_s).astype(jnp.bfloat16), swo_ref[...])

    off = 2 * d_conf + 3 * d_sc
    gates = _sigmoid(_dot(hb, w_in_ref[:, off:off + 2 * d_model]) + bgate_ref[...])
    merged = gates[:, :d_model] * y_a + gates[:, d_model:] * y_b
    x1 = x + gt1 * _dot(merged.astype(jnp.bfloat16), wo_ref[...])

    h2 = _rms_modulate(x1, n2g_ref[...], sh2, sc2).astype(jnp.bfloat16)
    ffn = None
    start = 0
    for width in FFN_CHUNKS:
        f_g = _dot(h2, wgu_ref[:, start:start + width])
        f_u = _dot(h2, wgu_ref[:, d_ff + start:d_ff + start + width])
        a = (f_g * _sigmoid(f_g) * f_u).astype(jnp.bfloat16)
        part = _dot(a, wdn_ref[start:start + width, :])
        ffn = part if ffn is None else ffn + part
        start += width
    x2 = x1 + gt2 * ffn

    ms = jnp.mean(x2 * x2, axis=-1, keepdims=True)
    o_ref[...] = x2 * lax.rsqrt(ms + RMS_EPS) * fg_ref[...]


def _resident(shape):
    nd = len(shape)
    return pl.BlockSpec(shape, lambda s: (0,) * nd, pipeline_mode=pl.Buffered(1))


def _adaln_mod(c, w_ada, b_ada):
    batch, d_model = c.shape
    n_mod = w_ada.shape[1] // d_model
    return pl.pallas_call(
        _mod_kernel,
        grid=(n_mod,),
        in_specs=[
            pl.BlockSpec((batch, d_model), lambda i: (0, 0)),
            pl.BlockSpec((d_model, d_model), lambda i: (0, i)),
            pl.BlockSpec((1, d_model), lambda i: (0, i)),
        ],
        out_specs=pl.BlockSpec((batch, d_model), lambda i: (0, i)),
        out_shape=jax.ShapeDtypeStruct((batch, n_mod * d_model), jnp.float32),
        name="adaln_mod",
    )(c, w_ada, b_ada.reshape(1, -1))


def _block(x, mod, norm1_g, w_in, b_glu, conf_dw_w, conf_dw_b, conf_ln_g, conf_ln_b,
           conf_w_out, conf_b_out, sc_dw_w, sc_w_out, b_gate, w_o, norm2_g, w_gu,
           w_down, final_g):
    batch, seq, d_model = x.shape
    d_conf = conf_dw_w.shape[1]
    d_sc = sc_dw_w.shape[1]
    d_ff = w_down.shape[0]
    tile = TOKEN_TILE
    assert seq % tile == 0 and sum(FFN_CHUNKS) == d_ff
    tiles_per_seq = seq // tile
    n_tiles = batch * tiles_per_seq
    halo_c = _round_up(conf_dw_w.shape[0] - 1, SUBLANES)
    halo_s = _round_up(sc_dw_w.shape[0] - 1, SUBLANES)

    bf = lambda w: w.astype(jnp.bfloat16)
    row = lambda v: v.reshape(1, -1)
    operands = [
        row(norm1_g), bf(w_in), row(b_glu), conf_dw_w, row(conf_dw_b), row(conf_ln_g),
        row(conf_ln_b), bf(conf_w_out), row(conf_b_out), sc_dw_w, bf(sc_w_out),
        row(b_gate), bf(w_o), row(norm2_g), bf(w_gu), bf(w_down), row(final_g),
    ]
    n_mod = mod.shape[1] // d_model
    mod4 = mod.reshape(batch, n_mod, 1, d_model)

    def front_tile(s):
        return jnp.minimum(s, n_tiles - 1)

    def back_tile(s):
        return jnp.maximum(s - 1, 0)

    def x_spec(which):
        return pl.BlockSpec(
            (None, tile, d_model),
            lambda s: (which(s) // tiles_per_seq, which(s) % tiles_per_seq, 0))

    def mod_spec(which):
        return pl.BlockSpec((None, n_mod, 1, d_model),
                            lambda s: (which(s) // tiles_per_seq, 0, 0, 0))

    kern = functools.partial(
        _block_kernel, tile=tile, tiles_per_seq=tiles_per_seq, n_tiles=n_tiles,
        d_model=d_model, d_conf=d_conf, d_sc=d_sc, d_ff=d_ff, halo_c=halo_c,
        halo_s=halo_s)
    return pl.pallas_call(
        kern,
        grid=(n_tiles + 1,),
        in_specs=[x_spec(front_tile), mod_spec(front_tile),
                  x_spec(back_tile), mod_spec(back_tile)]
                 + [_resident(op.shape) for op in operands],
        out_specs=x_spec(back_tile),
        out_shape=jax.ShapeDtypeStruct(x.shape, x.dtype),
        scratch_shapes=[
            pltpu.VMEM((halo_c + tile, d_conf), jnp.float32),
            pltpu.VMEM((halo_s + tile, d_sc), jnp.float32),
            pltpu.VMEM((tile, d_conf), jnp.bfloat16),
        ],
        compiler_params=pltpu.CompilerParams(
            dimension_semantics=("arbitrary",),
            vmem_limit_bytes=VMEM_LIMIT_BYTES),
        name="fused_block",
    )(x, mod4, x, mod4, *operands)


def kernel(x, c, w_ada, b_ada, norm1_g, w_in, b_glu, conf_dw_w, conf_dw_b, conf_ln_g,
           conf_ln_b, conf_w_out, conf_b_out, sc_dw_w, sc_w_out, b_gate, w_o, norm2_g,
           w_gu, w_down, final_g):
    depth = w_ada.shape[0]
    assert depth == 1, "final RMSNorm is fused into the (single) block"
    mod = _adaln_mod(c, w_ada[0], b_ada[0])
    return _block(x, mod, norm1_g[0], w_in[0], b_glu[0], conf_dw_w[0], conf_dw_b[0],
                  conf_ln_g[0], conf_ln_b[0], conf_w_out[0], conf_b_out[0], sc_dw_w[0],
                  sc_w_out[0], b_gate[0], w_o[0], norm2_g[0], w_gu[0], w_down[0],
                  final_g)
```
